```python
import jax, jax.numpy as jnp
from jax import lax
import numpy as np

D_MODEL = 1024
BATCH = 16
SEQ = 4096
DEPTH = 4

MEM_LEN = 256
CONV_WIDTH = 512
CONV_K = 3
HG_HEADS = 4
HG_F = 128
HG_I = 128
HG_QK = HG_HEADS * HG_F
HG_WIDTH = HG_HEADS * HG_I
HG_CHUNK = 32
MEM_HEADS = 4
MEM_HEAD_DIM = 128
MEM_WIDTH = MEM_HEADS * MEM_HEAD_DIM
N_BRANCH = 3
BRANCH_WIDTH = 512
D_FF = 4 * D_MODEL
ALPHA = (2.0 * DEPTH) ** 0.25
BETA = (8.0 * DEPTH) ** -0.25
LN_EPS = 1e-5
RMS_EPS = 1e-6
IN_SPLITS = (CONV_WIDTH, CONV_WIDTH, CONV_WIDTH, HG_QK, HG_QK, HG_WIDTH, HG_WIDTH, MEM_WIDTH, D_MODEL, D_MODEL, D_MODEL)
IN_COLS = sum(IN_SPLITS)

kernel_name = "hybrid_conv_hgrn2_memattn_postnorm"


def _layer_norm(x, g, b):
    xf = x.astype(jnp.float32)
    mu = jnp.mean(xf, axis=-1, keepdims=True)
    var = jnp.mean(jnp.square(xf - mu), axis=-1, keepdims=True)
    y = (xf - mu) * lax.rsqrt(var + LN_EPS) * g.astype(jnp.float32) + b.astype(jnp.float32)
    return y.astype(x.dtype)


def _short_conv_mixer(gate_b, gate_c, h, conv_w):
    u = gate_c * h
    seq = u.shape[1]
    u_pad = jnp.pad(u, ((0, 0), (CONV_K - 1, 0), (0, 0)))
    y = u_pad[:, 0:seq] * conv_w[0]
    for tap in range(1, CONV_K):
        y = y + u_pad[:, tap:tap + seq] * conv_w[tap]
    return gate_b * y


def _hgrn2_mixer(q, f_logit, i, g, lb, norm_w):
    bsz, seq, _ = q.shape
    n_chunk = seq // HG_CHUNK
    f32 = jnp.float32
    fl = f_logit.astype(f32)
    lb = lb.astype(f32)
    sig = jax.nn.sigmoid(fl)
    log_f = jnp.log(lb + (1.0 - lb) * sig)
    k = (1.0 - lb) * jax.nn.sigmoid(-fl)

    def to_chunks(t, d):
        return t.reshape(bsz, n_chunk, HG_CHUNK, HG_HEADS, d).transpose(0, 3, 1, 2, 4)

    qc = to_chunks(jax.nn.silu(q.astype(f32)), HG_F)
    kc = to_chunks(k, HG_F)
    vc = to_chunks(i.astype(f32), HG_I)
    bc = jnp.cumsum(to_chunks(log_f, HG_F), axis=3)
    b_ref = bc[:, :, :, HG_CHUNK // 2 - 1:HG_CHUNK // 2, :]
    b_last = bc[:, :, :, -1:, :]

    causal = jnp.tril(jnp.ones((HG_CHUNK, HG_CHUNK), dtype=bool))
    scores = jnp.einsum('bhntf,bhnsf->bhnts', qc * jnp.exp(bc - b_ref), kc * jnp.exp(b_ref - bc))
    scores = jnp.where(causal, scores, 0.0)
    o_intra = jnp.einsum('bhnts,bhnsv->bhntv', scores, vc)

    q_in = qc * jnp.exp(bc)
    k_out = kc * jnp.exp(b_last - bc)
    decay = jnp.exp(b_last[:, :, :, 0, :])

    def step(state, xs):
        q_n, k_n, v_n, dec_n = xs
        o_n = jnp.einsum('bhtf,bhfv->bhtv', q_n, state)
        state = dec_n[..., None] * state + jnp.einsum('bhsf,bhsv->bhfv', k_n, v_n)
        return state, o_n

    xs = (jnp.moveaxis(q_in, 2, 0), jnp.moveaxis(k_out, 2, 0), jnp.moveaxis(vc, 2, 0), jnp.moveaxis(decay, 2, 0))
    init = jnp.zeros((bsz, HG_HEADS, HG_F, HG_I), f32)
    _, o_inter = lax.scan(step, init, xs)
    o = o_intra + jnp.moveaxis(o_inter, 0, 2)

    o = o.transpose(0, 2, 3, 1, 4).reshape(bsz, seq, HG_HEADS, HG_I)
    o = o * lax.rsqrt(jnp.mean(o * o, axis=-1, keepdims=True) + RMS_EPS) * norm_w.astype(f32)
    o = o.reshape(bsz, seq, HG_WIDTH) * jax.nn.silu(g.astype(f32))
    return o.astype(q.dtype)


def _memory_attention(q, mem_k, mem_v):
    bsz, seq, _ = q.shape
    qh = q.reshape(bsz, seq, MEM_HEADS, MEM_HEAD_DIM)
    kh = mem_k.reshape(bsz, MEM_LEN, MEM_HEADS, MEM_HEAD_DIM)
    vh = mem_v.reshape(bsz, MEM_LEN, MEM_HEADS, MEM_HEAD_DIM)
    s = jnp.einsum('bthd,bmhd->bhtm', qh, kh).astype(jnp.float32) * (MEM_HEAD_DIM ** -0.5)
    p = jax.nn.softmax(s, axis=-1).astype(vh.dtype)
    o = jnp.einsum('bhtm,bmhd->bthd', p, vh)
    return o.reshape(bsz, seq, MEM_WIDTH)


def _hybrid_layer(x, mem, lb, w_in, conv_w, hg_norm_w, w_mem_k, w_mem_v, w_branch, b_gate, w_o,
                  ln1_g, ln1_b, w_up, w_down, ln2_g, ln2_b):
    bsz, seq, _ = x.shape
    proj = jnp.einsum('bsd,dc->bsc', x, w_in)
    split_idx = [int(v) for v in np.cumsum(IN_SPLITS)[:-1]]
    cb, cc, ch, hq, hf, hi, hg, mq, ga, gb, gc = jnp.split(proj, split_idx, axis=-1)

    y_a = _short_conv_mixer(cb, cc, ch, conv_w)
    y_b = _hgrn2_mixer(hq, hf, hi, hg, lb, hg_norm_w)
    y_c = _memory_attention(mq, jnp.einsum('bmd,dw->bmw', mem, w_mem_k), jnp.einsum('bmd,dw->bmw', mem, w_mem_v))

    b_ga, b_gb, b_gc = jnp.split(b_gate, N_BRANCH, axis=-1)
    merged = (jax.nn.sigmoid(ga + b_ga) * jnp.einsum('bsw,wd->bsd', y_a, w_branch[0])
              + jax.nn.sigmoid(gb + b_gb) * jnp.einsum('bsw,wd->bsd', y_b, w_branch[1])
              + jax.nn.sigmoid(gc + b_gc) * jnp.einsum('bsw,wd->bsd', y_c, w_branch[2]))
    mixed = jnp.einsum('bsd,de->bse', merged, w_o)
    x = _layer_norm(ALPHA * x + mixed, ln1_g, ln1_b)

    h = jnp.square(jax.nn.relu(jnp.einsum('bsd,df->bsf', x, w_up)))
    x = _layer_norm(ALPHA * x + jnp.einsum('bsf,fd->bsd', h, w_down), ln2_g, ln2_b)
    return x


def setup_inputs(seed: int = 0) -> dict:
    key = jax.random.key(seed)
    ks = jax.random.split(key, 20)
    nrm = jax.random.normal
    f32 = jnp.float32
    return {
        'x': nrm(ks[0], (BATCH, SEQ, D_MODEL), f32),
        'mem': nrm(ks[1], (BATCH, MEM_LEN, D_MODEL), f32),
        'lower_bounds': 0.02 * nrm(ks[2], (DEPTH, HG_QK), f32),
        'w_in': nrm(ks[3], (DEPTH, D_MODEL, IN_COLS), f32) * D_MODEL ** -0.5,
        'conv_w': nrm(ks[4], (DEPTH, CONV_K, CONV_WIDTH), f32) * CONV_K ** -0.5,
        'hg_norm_w': 1.0 + 0.02 * nrm(ks[5], (DEPTH, HG_I), f32),
        'w_mem_k': nrm(ks[6], (DEPTH, D_MODEL, MEM_WIDTH), f32) * D_MODEL ** -0.5,
        'w_mem_v': nrm(ks[7], (DEPTH, D_MODEL, MEM_WIDTH), f32) * (D_MODEL ** -0.5 * BETA),
        'w_branch': nrm(ks[8], (DEPTH, N_BRANCH, BRANCH_WIDTH, D_MODEL), f32) * (BRANCH_WIDTH ** -0.5 * BETA),
        'b_gate': 0.02 * nrm(ks[9], (DEPTH, N_BRANCH * D_MODEL), f32),
        'w_o': nrm(ks[10], (DEPTH, D_MODEL, D_MODEL), f32) * (D_MODEL ** -0.5 * BETA),
        'ln1_g': 1.0 + 0.02 * nrm(ks[11], (DEPTH, D_MODEL), f32),
        'ln1_b': 0.02 * nrm(ks[12], (DEPTH, D_MODEL), f32),
        'w_up': nrm(ks[13], (DEPTH, D_MODEL, D_FF), f32) * (D_MODEL ** -0.5 * BETA),
        'w_down': nrm(ks[14], (DEPTH, D_FF, D_MODEL), f32) * (D_FF ** -0.5 * BETA),
        'ln2_g': 1.0 + 0.02 * nrm(ks[15], (DEPTH, D_MODEL), f32),
        'ln2_b': 0.02 * nrm(ks[16], (DEPTH, D_MODEL), f32),
    }


def reference(x, mem, lower_bounds, w_in, conv_w, hg_norm_w, w_mem_k, w_mem_v, w_branch, b_gate, w_o,
              ln1_g, ln1_b, w_up, w_down, ln2_g, ln2_b):
    lb_soft = jax.nn.softmax(lower_bounds.astype(jnp.float32), axis=0)
    lb_all = jnp.cumsum(lb_soft, axis=0) - lb_soft[0:1]
    for layer in range(DEPTH):
        x = _hybrid_layer(x, mem, lb_all[layer], w_in[layer], conv_w[layer], hg_norm_w[layer],
                          w_mem_k[layer], w_mem_v[layer], w_branch[layer], b_gate[layer], w_o[layer],
                          ln1_g[layer], ln1_b[layer], w_up[layer], w_down[layer], ln2_g[layer], ln2_b[layer])
    return x
```

```python
import functools

import jax
import jax.numpy as jnp
from jax import lax
from jax.experimental import pallas as pl
from jax.experimental.pallas import tpu as pltpu

F32 = jnp.float32
BF16 = jnp.bfloat16

D_MODEL = 1024
DEPTH = 4
MEM_LEN = 256
CONV_WIDTH = 512
CONV_K = 3
HG_HEADS = 4
HG_F = 128
HG_I = 128
HG_QK = HG_HEADS * HG_F
HG_WIDTH = HG_HEADS * HG_I
HG_CHUNK = 32
MEM_HEADS = 4
MEM_HEAD_DIM = 128
MEM_WIDTH = MEM_HEADS * MEM_HEAD_DIM
N_BRANCH = 3
BRANCH_WIDTH = 512
D_FF = 4 * D_MODEL
ALPHA = (2.0 * DEPTH) ** 0.25
LN_EPS = 1e-5
RMS_EPS = 1e-6

CONV_COLS = 3 * CONV_WIDTH
HG_COLS = 2 * HG_QK + 2 * HG_WIDTH
GATE_COLS = N_BRANCH * D_MODEL
OFF_HG = CONV_COLS
OFF_MQ = OFF_HG + HG_COLS
OFF_GATE = OFF_MQ + MEM_WIDTH
IN_COLS = OFF_GATE + GATE_COLS

SUB = 256
CONV_HALO = 8
VMEM_LIMIT_BYTES = 60 * 1024 * 1024

_NT = (((1,), (1,)), ((), ()))
_TN = (((0,), (0,)), ((), ()))


def _dot(a, b):
    return jnp.dot(a, b, preferred_element_type=F32)


def _layer_norm(z, g, b):
    mu = jnp.mean(z, axis=-1, keepdims=True)
    zc = z - mu
    var = jnp.mean(zc * zc, axis=-1, keepdims=True)
    return zc * lax.rsqrt(var + LN_EPS) * g + b


def _split3(v):
    h1 = v.astype(BF16)
    r1 = v - h1.astype(F32)
    h2 = r1.astype(BF16)
    h3 = (r1 - h2.astype(F32)).astype(BF16)
    return h1, h2, h3


def _const_block(shape, index_map=None):
    if index_map is None:
        nd = len(shape)
        index_map = lambda *_: (0,) * nd
    return pl.BlockSpec(shape, index_map, pipeline_mode=pl.Buffered(1))


def _kv_kernel(mem_ref, wk_ref, wv_ref, k_ref, v_ref):
    m = mem_ref[...].astype(BF16)
    k_ref[0] = _dot(m, wk_ref[0]).astype(BF16)
    v_ref[0] = _dot(m, wv_ref[0]).astype(BF16)


def _memory_kv(mem2d, wk, wv):
    rows = mem2d.shape[0]
    tm = min(rows, 1024)
    out = jax.ShapeDtypeStruct((DEPTH, rows, MEM_WIDTH), BF16)
    return pl.pallas_call(
        _kv_kernel,
        grid=(DEPTH, rows // tm),
        in_specs=[
            pl.BlockSpec((tm, D_MODEL), lambda l, i: (i, 0)),
            pl.BlockSpec((1, D_MODEL, MEM_WIDTH), lambda l, i: (l, 0, 0)),
            pl.BlockSpec((1, D_MODEL, MEM_WIDTH), lambda l, i: (l, 0, 0)),
        ],
        out_specs=[
            pl.BlockSpec((1, tm, MEM_WIDTH), lambda l, i: (l, i, 0)),
            pl.BlockSpec((1, tm, MEM_WIDTH), lambda l, i: (l, i, 0)),
        ],
        out_shape=[out, out],
        compiler_params=pltpu.CompilerParams(dimension_semantics=("arbitrary", "arbitrary")),
        name="memory_kv",
    )(mem2d, wk, wv)


def _chunk_matrices():
    t = lax.broadcasted_iota(jnp.int32, (SUB, SUB), 0)
    s = lax.broadcasted_iota(jnp.int32, (SUB, SUB), 1)
    same = (t // HG_CHUNK) == (s // HG_CHUNK)
    causal = same & (s <= t)
    mid = (t // HG_CHUNK) * HG_CHUNK + (HG_CHUNK // 2 - 1)
    upto_mid = same & (s <= mid)
    cum = causal.astype(F32)
    dref = cum - upto_mid.astype(F32)
    rest = (same & (s > t)).astype(F32)
    return causal, cum.astype(BF16), dref.astype(BF16), rest.astype(BF16)


def _mix_kernel(layer, ts,
                x_ref, k_ref, v_ref, lbs_ref, win_ref, cw_ref, nw_ref, wb_ref, bg_ref, wo_ref, g_ref, b_ref,
                o_ref,
                pc_ref, ph_ref, pm_ref, pg_ref, y_ref, oi_ref, ubuf_ref, st_ref):
    @pl.when(pl.program_id(1) == 0)
    def _():
        ubuf_ref[0:CONV_HALO, :] = jnp.zeros((CONV_HALO, CONV_WIDTH), F32)
        st_ref[...] = jnp.zeros(st_ref.shape, F32)

    x = x_ref[0]
    xb = x.astype(BF16)
    pc_ref[...] = _dot(xb, win_ref[:, 0:OFF_HG])
    ph_ref[...] = _dot(xb, win_ref[:, OFF_HG:OFF_MQ])
    pm_ref[...] = _dot(xb, win_ref[:, OFF_MQ:OFF_GATE]).astype(BF16)
    pg_ref[...] = _dot(xb, win_ref[:, OFF_GATE:IN_COLS])

    lbs = lbs_ref[...]
    e = jnp.exp(lbs - jnp.max(lbs, axis=0, keepdims=True))
    soft = e / jnp.sum(e, axis=0, keepdims=True)
    cs = soft[0:1]
    for j in range(1, layer + 1):
        cs = cs + soft[j:j + 1]
    lb = cs - soft[0:1]

    u = pc_ref[:, CONV_WIDTH:2 * CONV_WIDTH] * pc_ref[:, 2 * CONV_WIDTH:3 * CONV_WIDTH]
    ubuf_ref[CONV_HALO:CONV_HALO + ts, :] = u
    cw = cw_ref[...]
    conv = ubuf_ref[CONV_HALO - 2:CONV_HALO - 2 + ts, :] * cw[0:1]
    conv = conv + ubuf_ref[CONV_HALO - 1:CONV_HALO - 1 + ts, :] * cw[1:2]
    conv = conv + u * cw[2:3]
    y_ref[:, 0:CONV_WIDTH] = (pc_ref[:, 0:CONV_WIDTH] * conv).astype(BF16)
    ubuf_ref[0:CONV_HALO, :] = ubuf_ref[ts:ts + CONV_HALO, :]

    causal, m_cum, m_dref, m_rest = _chunk_matrices()
    nw = nw_ref[...]
    scale = MEM_HEAD_DIM ** -0.5

    for r0 in range(0, ts, SUB):
        rows = slice(r0, r0 + SUB)
        fl = ph_ref[rows, HG_QK:2 * HG_QK]
        ef = jnp.exp(-jnp.abs(fl))
        rf = 1.0 / (1.0 + ef)
        sig_pos = jnp.where(fl >= 0, rf, ef * rf)
        sig_neg = jnp.where(fl >= 0, ef * rf, rf)
        log_f = jnp.log(lb + (1.0 - lb) * sig_pos)
        kk = (1.0 - lb) * sig_neg
        parts = _split3(log_f)
        bc = _dot(m_cum, parts[0]) + _dot(m_cum, parts[1]) + _dot(m_cum, parts[2])
        d_ref = _dot(m_dref, parts[0]) + _dot(m_dref, parts[1]) + _dot(m_dref, parts[2])
        d_rest = _dot(m_rest, parts[0]) + _dot(m_rest, parts[1]) + _dot(m_rest, parts[2])
        for h in range(HG_HEADS):
            cols = slice(h * HG_F, (h + 1) * HG_F)
            q = ph_ref[rows, h * HG_F:(h + 1) * HG_F]
            sq = q * jax.nn.sigmoid(q)
            kh = kk[:, cols]
            bch = bc[:, cols]
            a = (sq * jnp.exp(d_ref[:, cols])).astype(BF16)
            bm = (kh * jnp.exp(-d_ref[:, cols])).astype(BF16)
            q_in = (sq * jnp.exp(bch)).astype(BF16)
            k_out = (kh * jnp.exp(d_rest[:, cols])).astype(BF16)
            vb = ph_ref[rows, 2 * HG_QK + h * HG_I:2 * HG_QK + (h + 1) * HG_I].astype(BF16)
            scores = lax.dot_general(a, bm, _NT, preferred_element_type=F32)
            scores = jnp.where(causal, scores, 0.0).astype(BF16)
            o_intra = _dot(scores, vb)
            state = st_ref[h]
            for n in range(SUB // HG_CHUNK):
                cr = slice(n * HG_CHUNK, (n + 1) * HG_CHUNK)
                oi_ref[r0 + n * HG_CHUNK:r0 + (n + 1) * HG_CHUNK, cols] = lax.dot_general(
                    q_in[cr], state.astype(BF16), _NT, preferred_element_type=F32)
                decay = jnp.exp(bch[(n + 1) * HG_CHUNK - 1:(n + 1) * HG_CHUNK])
                upd = lax.dot_general(vb[cr], k_out[cr], _TN, preferred_element_type=F32)
                state = state * decay + upd
            st_ref[h] = state
            o = o_intra + oi_ref[rows, cols]
            g = ph_ref[rows, 2 * HG_QK + HG_WIDTH + h * HG_I:2 * HG_QK + HG_WIDTH + (h + 1) * HG_I]
            o = o * lax.rsqrt(jnp.mean(o * o, axis=-1, keepdims=True) + RMS_EPS) * nw
            y_ref[rows, CONV_WIDTH + h * HG_I:CONV_WIDTH + (h + 1) * HG_I] = (o * (g * jax.nn.sigmoid(g))).astype(BF16)

        for h in range(MEM_HEADS):
            cols = slice(h * MEM_HEAD_DIM, (h + 1) * MEM_HEAD_DIM)
            s = lax.dot_general(pm_ref[rows, cols], k_ref[0, :, cols], _NT, preferred_element_type=F32) * scale
            p = jnp.exp(s - jnp.max(s, axis=-1, keepdims=True))
            denom = jnp.sum(p, axis=-1, keepdims=True)
            oc = _dot(p.astype(BF16), v_ref[0, :, cols]) * (1.0 / denom)
            y_ref[rows, CONV_WIDTH + HG_WIDTH + h * MEM_HEAD_DIM:CONV_WIDTH + HG_WIDTH + (h + 1) * MEM_HEAD_DIM] = oc.astype(BF16)

    bg = bg_ref[...]
    merged = None
    for j in range(N_BRANCH):
        gate = jax.nn.sigmoid(pg_ref[:, j * D_MODEL:(j + 1) * D_MODEL] + bg[:, j * D_MODEL:(j + 1) * D_MODEL])
        term = gate * _dot(y_ref[:, j * BRANCH_WIDTH:(j + 1) * BRANCH_WIDTH], wb_ref[j])
        merged = term if merged is None else merged + term
    mixed = _dot(merged.astype(BF16), wo_ref[...])
    o_ref[0] = _layer_norm(ALPHA * x + mixed, g_ref[...], b_ref[...])


def _mix_layer(layer, x, kk, vv, lower_bounds, w_in, conv_w, hg_norm_w, w_branch, b_gate, w_o, ln_g, ln_b, ts):
    bsz, seq, _ = x.shape
    row = lambda n: pl.BlockSpec((1, n), lambda b, s: (0, 0))
    return pl.pallas_call(
        functools.partial(_mix_kernel, layer, ts),
        grid=(bsz, seq // ts),
        in_specs=[
            pl.BlockSpec((1, ts, D_MODEL), lambda b, s: (b, s, 0)),
            pl.BlockSpec((1, MEM_LEN, MEM_WIDTH), lambda b, s: (b, 0, 0)),
            pl.BlockSpec((1, MEM_LEN, MEM_WIDTH), lambda b, s: (b, 0, 0)),
            pl.BlockSpec((DEPTH, HG_QK), lambda b, s: (0, 0)),
            _const_block((D_MODEL, IN_COLS)),
            pl.BlockSpec((CONV_K, CONV_WIDTH), lambda b, s: (0, 0)),
            row(HG_I),
            _const_block((N_BRANCH, BRANCH_WIDTH, D_MODEL)),
            row(GATE_COLS),
            _const_block((D_MODEL, D_MODEL)),
            row(D_MODEL),
            row(D_MODEL),
        ],
        out_specs=pl.BlockSpec((1, ts, D_MODEL), lambda b, s: (b, s, 0)),
        out_shape=jax.ShapeDtypeStruct(x.shape, F32),
        scratch_shapes=[
            pltpu.VMEM((ts, CONV_COLS), F32),
            pltpu.VMEM((ts, HG_COLS), F32),
            pltpu.VMEM((ts, MEM_WIDTH), BF16),
            pltpu.VMEM((ts, GATE_COLS), F32),
            pltpu.VMEM((ts, N_BRANCH * BRANCH_WIDTH), BF16),
            pltpu.VMEM((ts, HG_WIDTH), F32),
            pltpu.VMEM((ts + CONV_HALO, CONV_WIDTH), F32),
            pltpu.VMEM((HG_HEADS, HG_I, HG_F), F32),
        ],
        compiler_params=pltpu.CompilerParams(
            dimension_semantics=("arbitrary", "arbitrary"), vmem_limit_bytes=VMEM_LIMIT_BYTES),
        name=f"mix_layer{layer}",
    )(x, kk, vv, lower_bounds, w_in, conv_w, hg_norm_w, w_branch, b_gate, w_o, ln_g, ln_b)


def _mlp_kernel(x_ref, wu_ref, wd_ref, g_ref, b_ref, o_ref):
    x = x_ref[...]
    h = _dot(x.astype(BF16), wu_ref[...])
    h = jnp.square(jnp.maximum(h, 0.0)).astype(BF16)
    y = _dot(h, wd_ref[...])
    o_ref[...] = _layer_norm(ALPHA * x + y, g_ref[...], b_ref[...])


def _mlp_layer(layer, x2d, w_up, w_down, ln_g, ln_b, tm):
    rows = x2d.shape[0]
    row = lambda n: pl.BlockSpec((1, n), lambda i: (0, 0))
    return pl.pallas_call(
        _mlp_kernel,
        grid=(rows // tm,),
        in_specs=[
            pl.BlockSpec((tm, D_MODEL), lambda i: (i, 0)),
            _const_block((D_MODEL, D_FF)),
            _const_block((D_FF, D_MODEL)),
            row(D_MODEL),
            row(D_MODEL),
        ],
        out_specs=pl.BlockSpec((tm, D_MODEL), lambda i: (i, 0)),
        out_shape=jax.ShapeDtypeStruct(x2d.shape, F32),
        compiler_params=pltpu.CompilerParams(
            dimension_semantics=("arbitrary",), vmem_limit_bytes=VMEM_LIMIT_BYTES),
        name=f"mlp_layer{layer}",
    )(x2d, w_up, w_down, ln_g, ln_b)


def _tile_rows(seq):
    for t in (512, 256):
        if seq % t == 0:
            return t
    raise ValueError(f"sequence length {seq} must be a multiple of {SUB}")


def kernel(x, mem, lower_bounds, w_in, conv_w, hg_norm_w, w_mem_k, w_mem_v, w_branch, b_gate, w_o, ln1_g, ln1_b, w_up, w_down, ln2_g, ln2_b):
    bsz, seq, _ = x.shape
    ts = _tile_rows(seq)
    kk, vv = _memory_kv(mem.reshape(bsz * MEM_LEN, D_MODEL), w_mem_k.astype(BF16), w_mem_v.astype(BF16))
    kk = kk.reshape(DEPTH, bsz, MEM_LEN, MEM_WIDTH)
    vv = vv.reshape(DEPTH, bsz, MEM_LEN, MEM_WIDTH)
    for layer in range(DEPTH):
        x = _mix_layer(layer, x, kk[layer], vv[layer], lower_bounds, w_in[layer].astype(BF16), conv_w[layer],
                       hg_norm_w[layer][None], w_branch[layer].astype(BF16), b_gate[layer][None],
                       w_o[layer].astype(BF16), ln1_g[layer][None], ln1_b[layer][None], ts)
        x = _mlp_layer(layer, x.reshape(bsz * seq, D_MODEL), w_up[layer].astype(BF16), w_down[layer].astype(BF16),
                       ln2_g[layer][None], ln2_b[layer][None], ts).reshape(bsz, seq, D_MODEL)
    return x
```

```python
import functools

import jax
import jax.numpy as jnp
from jax import lax
from jax.experimental import pallas as pl
from jax.experimental.pallas import tpu as pltpu

F32 = jnp.float32
BF16 = jnp.bfloat16

D_MODEL = 1024
DEPTH = 4
MEM_LEN = 256
CONV_WIDTH = 512
CONV_K = 3
HG_HEADS = 4
HG_F = 128
HG_I = 128
HG_QK = HG_HEADS * HG_F
HG_WIDTH = HG_HEADS * HG_I
HG_CHUNK = 32
MEM_HEADS = 4
MEM_HEAD_DIM = 128
MEM_WIDTH = MEM_HEADS * MEM_HEAD_DIM
N_BRANCH = 3
BRANCH_WIDTH = 512
D_FF = 4 * D_MODEL
ALPHA = (2.0 * DEPTH) ** 0.25
LN_EPS = 1e-5
RMS_EPS = 1e-6

CONV_COLS = 3 * CONV_WIDTH
HG_COLS = 2 * HG_QK + 2 * HG_WIDTH
GATE_COLS = N_BRANCH * D_MODEL
OFF_HG = CONV_COLS
OFF_MQ = OFF_HG + HG_COLS
OFF_GATE = OFF_MQ + MEM_WIDTH
IN_COLS = OFF_GATE + GATE_COLS

SUB = 256
SUB_CHUNKS = SUB // HG_CHUNK
CONV_HALO = 8
VMEM_LIMIT_BYTES = 60 * 1024 * 1024

_HA, _HB, _HQ, _HK, _HV = (i * HG_QK for i in range(5))

_NT = (((1,), (1,)), ((), ()))
_TN = (((0,), (0,)), ((), ()))


def _dot(a, b):
    return jnp.dot(a, b, preferred_element_type=F32)


def _layer_norm(z, g, b):
    mu = jnp.mean(z, axis=-1, keepdims=True)
    zc = z - mu
    var = jnp.mean(zc * zc, axis=-1, keepdims=True)
    return zc * lax.rsqrt(var + LN_EPS) * g + b


def _split3(v):
    h1 = v.astype(BF16)
    r1 = v - h1.astype(F32)
    h2 = r1.astype(BF16)
    h3 = (r1 - h2.astype(F32)).astype(BF16)
    return h1, h2, h3


def _const_block(shape, index_map=None):
    if index_map is None:
        nd = len(shape)
        index_map = lambda *_: (0,) * nd
    return pl.BlockSpec(shape, index_map, pipeline_mode=pl.Buffered(1))


def _kv_kernel(mem_ref, wk_ref, wv_ref, k_ref, v_ref):
    m = mem_ref[...].astype(BF16)
    k_ref[0] = _dot(m, wk_ref[0]).astype(BF16)
    v_ref[0] = _dot(m, wv_ref[0]).astype(BF16)


def _memory_kv(mem2d, wk, wv):
    rows = mem2d.shape[0]
    tm = min(rows, 1024)
    out = jax.ShapeDtypeStruct((DEPTH, rows, MEM_WIDTH), BF16)
    return pl.pallas_call(
        _kv_kernel,
        grid=(DEPTH, rows // tm),
        in_specs=[
            pl.BlockSpec((tm, D_MODEL), lambda l, i: (i, 0)),
            pl.BlockSpec((1, D_MODEL, MEM_WIDTH), lambda l, i: (l, 0, 0)),
            pl.BlockSpec((1, D_MODEL, MEM_WIDTH), lambda l, i: (l, 0, 0)),
        ],
        out_specs=[
            pl.BlockSpec((1, tm, MEM_WIDTH), lambda l, i: (l, i, 0)),
            pl.BlockSpec((1, tm, MEM_WIDTH), lambda l, i: (l, i, 0)),
        ],
        out_shape=[out, out],
        compiler_params=pltpu.CompilerParams(dimension_semantics=("arbitrary", "arbitrary")),
        name="memory_kv",
    )(mem2d, wk, wv)


def _chunk_matrices():
    t = lax.broadcasted_iota(jnp.int32, (SUB, SUB), 0)
    s = lax.broadcasted_iota(jnp.int32, (SUB, SUB), 1)
    causal = ((t // HG_CHUNK) == (s // HG_CHUNK)) & (s <= t)
    return causal, causal.astype(F32).astype(BF16)


def _mix_kernel(layer, ts,
                x_ref, k_ref, v_ref, lbs_ref, win_ref, cw_ref, nw_ref, wb_ref, bg_ref, wo_ref, g_ref, b_ref,
                o_ref,
                pc_ref, ph_ref, pm_ref, pg_ref, y_ref, hop_ref, dec_ref, u_ref, sst_ref, ubuf_ref, st_ref):
    @pl.when(pl.program_id(1) == 0)
    def _():
        ubuf_ref[0:CONV_HALO, :] = jnp.zeros((CONV_HALO, CONV_WIDTH), F32)
        st_ref[...] = jnp.zeros(st_ref.shape, F32)

    bc_ref = pc_ref.at[:, 0:HG_QK]
    oin_ref = pc_ref.at[:, HG_QK:2 * HG_QK]
    oi_ref = pc_ref.at[:, 2 * HG_QK:3 * HG_QK]

    x = x_ref[0]
    xb = x.astype(BF16)
    ph_ref[...] = _dot(xb, win_ref[:, OFF_HG:OFF_MQ])
    pc_ref[...] = _dot(xb, win_ref[:, 0:OFF_HG])
    pm_ref[...] = _dot(xb, win_ref[:, OFF_MQ:OFF_GATE]).astype(BF16)

    lbs = lbs_ref[...]
    e = jnp.exp(lbs - jnp.max(lbs, axis=0, keepdims=True))
    soft = e / jnp.sum(e, axis=0, keepdims=True)
    cs = soft[0:1]
    for j in range(1, layer + 1):
        cs = cs + soft[j:j + 1]
    lb = cs - soft[0:1]

    u = pc_ref[:, CONV_WIDTH:2 * CONV_WIDTH] * pc_ref[:, 2 * CONV_WIDTH:3 * CONV_WIDTH]
    ubuf_ref[CONV_HALO:CONV_HALO + ts, :] = u
    cw = cw_ref[...]
    conv = ubuf_ref[CONV_HALO - 2:CONV_HALO - 2 + ts, :] * cw[0:1]
    conv = conv + ubuf_ref[CONV_HALO - 1:CONV_HALO - 1 + ts, :] * cw[1:2]
    conv = conv + u * cw[2:3]
    y_ref[:, 0:CONV_WIDTH] = (pc_ref[:, 0:CONV_WIDTH] * conv).astype(BF16)
    ubuf_ref[0:CONV_HALO, :] = ubuf_ref[ts:ts + CONV_HALO, :]

    causal, m_cum = _chunk_matrices()
    nw = nw_ref[...]
    scale = MEM_HEAD_DIM ** -0.5

    def hgrn_elementwise(r0):
        rows = slice(r0, r0 + SUB)
        fl = ph_ref[rows, HG_QK:2 * HG_QK]
        ef = jnp.exp(-jnp.abs(fl))
        rf = 1.0 / (1.0 + ef)
        sig_pos = jnp.where(fl >= 0, rf, ef * rf)
        sig_neg = jnp.where(fl >= 0, ef * rf, rf)
        log_f = jnp.log(lb + (1.0 - lb) * sig_pos)
        kk = (1.0 - lb) * sig_neg
        parts = _split3(log_f)
        bc = _dot(m_cum, parts[0]) + _dot(m_cum, parts[1]) + _dot(m_cum, parts[2])
        bc_ref[rows, :] = bc
        b_mid, b_end = [], []
        for n in range(SUB_CHUNKS):
            c0 = r0 + n * HG_CHUNK
            mid = c0 + HG_CHUNK // 2 - 1
            last = c0 + HG_CHUNK - 1
            b_mid.append(jnp.broadcast_to(bc_ref[mid:mid + 1, :], (HG_CHUNK, HG_QK)))
            b_last = bc_ref[last:last + 1, :]
            b_end.append(jnp.broadcast_to(b_last, (HG_CHUNK, HG_QK)))
            dec_ref[c0 // HG_CHUNK:c0 // HG_CHUNK + 1, :] = jnp.exp(b_last)
        b_mid = jnp.concatenate(b_mid, axis=0)
        b_end = jnp.concatenate(b_end, axis=0)
        q = ph_ref[rows, 0:HG_QK]
        sq = q * jax.nn.sigmoid(q)
        hop_ref[rows, _HA:_HA + HG_QK] = (sq * jnp.exp(bc - b_mid)).astype(BF16)
        hop_ref[rows, _HB:_HB + HG_QK] = (kk * jnp.exp(b_mid - bc)).astype(BF16)
        hop_ref[rows, _HQ:_HQ + HG_QK] = (sq * jnp.exp(bc)).astype(BF16)
        hop_ref[rows, _HK:_HK + HG_QK] = (kk * jnp.exp(b_end - bc)).astype(BF16)
        hop_ref[rows, _HV:_HV + HG_WIDTH] = ph_ref[rows, 2 * HG_QK:2 * HG_QK + HG_WIDTH].astype(BF16)

    def hgrn_matmuls(r0):
        rows = slice(r0, r0 + SUB)
        for h in range(HG_HEADS):
            scores = lax.dot_general(hop_ref[rows, _HA + h * HG_F:_HA + (h + 1) * HG_F],
                                     hop_ref[rows, _HB + h * HG_F:_HB + (h + 1) * HG_F], _NT, preferred_element_type=F32)
            scores = jnp.where(causal, scores, 0.0).astype(BF16)
            oin_ref[rows, h * HG_I:(h + 1) * HG_I] = _dot(scores, hop_ref[rows, _HV + h * HG_I:_HV + (h + 1) * HG_I])
        for n in range(SUB_CHUNKS):
            c0 = r0 + n * HG_CHUNK
            cr = slice(c0, c0 + HG_CHUNK)
            for h in range(HG_HEADS):
                u_ref[c0 // HG_CHUNK, h] = lax.dot_general(
                    hop_ref[cr, _HV + h * HG_I:_HV + (h + 1) * HG_I], hop_ref[cr, _HK + h * HG_F:_HK + (h + 1) * HG_F],
                    _TN, preferred_element_type=F32)

    def attention(r0):
        rows = slice(r0, r0 + SUB)
        for h in range(MEM_HEADS):
            cols = slice(h * MEM_HEAD_DIM, (h + 1) * MEM_HEAD_DIM)
            s = lax.dot_general(pm_ref[rows, cols], k_ref[0, :, cols], _NT, preferred_element_type=F32) * scale
            p = jnp.exp(s - jnp.max(s, axis=-1, keepdims=True))
            denom = jnp.sum(p, axis=-1, keepdims=True)
            oc = _dot(p.astype(BF16), v_ref[0, :, cols]) * (1.0 / denom)
            y_ref[rows, CONV_WIDTH + HG_WIDTH + h * MEM_HEAD_DIM:CONV_WIDTH + HG_WIDTH + (h + 1) * MEM_HEAD_DIM] = oc.astype(BF16)

    def hgrn_recurrence(r0):
        for h in range(HG_HEADS):
            state = st_ref[h]
            for n in range(SUB_CHUNKS):
                c = r0 // HG_CHUNK + n
                sst_ref[c, h] = state.astype(BF16)
                state = state * dec_ref[c:c + 1, h * HG_F:(h + 1) * HG_F] + u_ref[c, h]
            st_ref[h] = state

    def hgrn_inter(r0):
        for n in range(SUB_CHUNKS):
            c0 = r0 + n * HG_CHUNK
            cr = slice(c0, c0 + HG_CHUNK)
            for h in range(HG_HEADS):
                oi_ref[cr, h * HG_I:(h + 1) * HG_I] = lax.dot_general(
                    hop_ref[cr, _HQ + h * HG_F:_HQ + (h + 1) * HG_F], sst_ref[c0 // HG_CHUNK, h], _NT,
                    preferred_element_type=F32)

    def hgrn_norm(r0):
        rows = slice(r0, r0 + SUB)
        for h in range(HG_HEADS):
            cols = slice(h * HG_I, (h + 1) * HG_I)
            o = oin_ref[rows, cols] + oi_ref[rows, cols]
            g = ph_ref[rows, 2 * HG_QK + HG_WIDTH + h * HG_I:2 * HG_QK + HG_WIDTH + (h + 1) * HG_I]
            o = o * lax.rsqrt(jnp.mean(o * o, axis=-1, keepdims=True) + RMS_EPS) * nw
            y_ref[rows, CONV_WIDTH + h * HG_I:CONV_WIDTH + (h + 1) * HG_I] = (o * (g * jax.nn.sigmoid(g))).astype(BF16)

    def gate_logits(j):
        cols = slice(j * D_MODEL, (j + 1) * D_MODEL)
        pg_ref[:, cols] = _dot(xb, win_ref[:, OFF_GATE + j * D_MODEL:OFF_GATE + (j + 1) * D_MODEL])

    subs = list(range(0, ts, SUB))
    for r0 in subs:
        hgrn_elementwise(r0)
    gate_logits(0)
    for r0 in subs:
        hgrn_matmuls(r0)
        attention(r0)
    gate_logits(1)
    for r0 in subs:
        hgrn_recurrence(r0)
        hgrn_inter(r0)
    gate_logits(2)
    for r0 in subs:
        hgrn_norm(r0)

    bg = bg_ref[...]
    merged = None
    for j in range(N_BRANCH):
        gate = jax.nn.sigmoid(pg_ref[:, j * D_MODEL:(j + 1) * D_MODEL] + bg[:, j * D_MODEL:(j + 1) * D_MODEL])
        term = gate * _dot(y_ref[:, j * BRANCH_WIDTH:(j + 1) * BRANCH_WIDTH], wb_ref[j])
        merged = term if merged is None else merged + term
    mixed = _dot(merged.astype(BF16), wo_ref[...])
    o_ref[0] = _layer_norm(ALPHA * x + mixed, g_ref[...], b_ref[...])


def _mix_layer(layer, x, kk, vv, lower_bounds, w_in, conv_w, hg_norm_w, w_branch, b_gate, w_o, ln_g, ln_b, ts):
    bsz, seq, _ = x.shape
    row = lambda n: pl.BlockSpec((1, n), lambda b, s: (0, 0))
    return pl.pallas_call(
        functools.partial(_mix_kernel, layer, ts),
        grid=(bsz, seq // ts),
        in_specs=[
            pl.BlockSpec((1, ts, D_MODEL), lambda b, s: (b, s, 0)),
            pl.BlockSpec((1, MEM_LEN, MEM_WIDTH), lambda b, s: (b, 0, 0)),
            pl.BlockSpec((1, MEM_LEN, MEM_WIDTH), lambda b, s: (b, 0, 0)),
            pl.BlockSpec((DEPTH, HG_QK), lambda b, s: (0, 0)),
            _const_block((D_MODEL, IN_COLS)),
            pl.BlockSpec((CONV_K, CONV_WIDTH), lambda b, s: (0, 0)),
            row(HG_I),
            _const_block((N_BRANCH, BRANCH_WIDTH, D_MODEL)),
            row(GATE_COLS),
            _const_block((D_MODEL, D_MODEL)),
            row(D_MODEL),
            row(D_MODEL),
        ],
        out_specs=pl.BlockSpec((1, ts, D_MODEL), lambda b, s: (b, s, 0)),
        out_shape=jax.ShapeDtypeStruct(x.shape, F32),
        scratch_shapes=[
            pltpu.VMEM((ts, CONV_COLS), F32),
            pltpu.VMEM((ts, HG_COLS), F32),
            pltpu.VMEM((ts, MEM_WIDTH), BF16),
            pltpu.VMEM((ts, GATE_COLS), F32),
            pltpu.VMEM((ts, N_BRANCH * BRANCH_WIDTH), BF16),
            pltpu.VMEM((ts, 5 * HG_QK), BF16),
            pltpu.VMEM((ts // HG_CHUNK, HG_QK), F32),
            pltpu.VMEM((ts // HG_CHUNK, HG_HEADS, HG_I, HG_F), F32),
            pltpu.VMEM((ts // HG_CHUNK, HG_HEADS, HG_I, HG_F), BF16),
            pltpu.VMEM((ts + CONV_HALO, CONV_WIDTH), F32),
            pltpu.VMEM((HG_HEADS, HG_I, HG_F), F32),
        ],
        compiler_params=pltpu.CompilerParams(
            dimension_semantics=("arbitrary", "arbitrary"), vmem_limit_bytes=VMEM_LIMIT_BYTES),
        name=f"mix_layer{layer}",
    )(x, kk, vv, lower_bounds, w_in, conv_w, hg_norm_w, w_branch, b_gate, w_o, ln_g, ln_b)


def _mlp_kernel(x_ref, wu_ref, wd_ref, g_ref, b_ref, o_ref):
    x = x_ref[...]
    h = _dot(x.astype(BF16), wu_ref[...])
    h = jnp.square(jnp.maximum(h, 0.0)).astype(BF16)
    y = _dot(h, wd_ref[...])
    o_ref[...] = _layer_norm(ALPHA * x + y, g_ref[...], b_ref[...])


def _mlp_layer(layer, x2d, w_up, w_down, ln_g, ln_b, tm):
    rows = x2d.shape[0]
    row = lambda n: pl.BlockSpec((1, n), lambda i: (0, 0))
    return pl.pallas_call(
        _mlp_kernel,
        grid=(rows // tm,),
        in_specs=[
            pl.BlockSpec((tm, D_MODEL), lambda i: (i, 0)),
            _const_block((D_MODEL, D_FF)),
            _const_block((D_FF, D_MODEL)),
            row(D_MODEL),
            row(D_MODEL),
        ],
        out_specs=pl.BlockSpec((tm, D_MODEL), lambda i: (i, 0)),
        out_shape=jax.ShapeDtypeStruct(x2d.shape, F32),
        compiler_params=pltpu.CompilerParams(
            dimension_semantics=("arbitrary",), vmem_limit_bytes=VMEM_LIMIT_BYTES),
        name=f"mlp_layer{layer}",
    )(x2d, w_up, w_down, ln_g, ln_b)


def _tile_rows(seq):
    for t in (512, 256):
        if seq % t == 0:
            return t
    raise ValueError(f"sequence length {seq} must be a multiple of {SUB}")


def kernel(x, mem, lower_bounds, w_in, conv_w, hg_norm_w, w_mem_k, w_mem_v, w_branch, b_gate, w_o, ln1_g, ln1_b, w_up, w_down, ln2_g, ln2_b):
    bsz, seq, _ = x.shape
    ts = _tile_rows(seq)
    kk, vv = _memory_kv(mem.reshape(bsz * MEM_LEN, D_MODEL), w_mem_k.astype(BF16), w_mem_v.astype(BF16))
    kk = kk.reshape(DEPTH, bsz, MEM_LEN, MEM_WIDTH)
    vv = vv.reshape(DEPTH, bsz, MEM_LEN, MEM_WIDTH)
    for layer in range(DEPTH):
        x = _mix_layer(layer, x, kk[layer], vv[layer], lower_bounds, w_in[layer].astype(BF16), conv_w[layer],
                       hg_norm_w[layer][None], w_branch[layer].astype(BF16), b_gate[layer][None],
                       w_o[layer].astype(BF16), ln1_g[layer][None], ln1_b[layer][None], ts)
        x = _mlp_layer(layer, x.reshape(bsz * seq, D_MODEL), w_up[layer].astype(BF16), w_down[layer].astype(BF16),
                       ln2_g[layer][None], ln2_b[layer][None], ts).reshape(bsz, seq, D_MODEL)
    return x
```
